```python
import math
import jax, jax.numpy as jnp
from jax import lax
import numpy as np


D_MODEL = 1024
BATCH = 2
SEQ = 8192
DEPTH = 2
DEC_BATCH = 128
DEC_SEQ = 1
PAST_LEN = 8192
PAGE_SIZE = 128

NSA_HEADS = 8
NSA_KV = 2
NSA_HD = 64
NSA_BLOCK = 64
NSA_TOPN = 16
WINDOW = 512
MLA_HEADS = 8
MLA_Q_RANK = 384
MLA_KV_RANK = 256
MLA_NOPE = 64
MLA_ROPE = 32
MLA_V = 64
SB_HEADS = 8
SB_KV = 4
SB_HD = 64
N_BRANCH = 3
BRANCH_W = 512
N_MEM = 256
X_HEADS = 4
X_HD = 128
D_FF = ((8 * D_MODEL // 3 + 127) // 128) * 128
CONV_W = 3

ROPE_THETA = 10000.0
EPS = 1e-6
Q_BLOCK = 128
NEG = -1e30
FORCE = 1e9

IN_SIZES = (NSA_HEADS * NSA_HD, 2 * NSA_KV * NSA_HD, 2 * NSA_KV * NSA_HD, 2 * NSA_KV * NSA_HD, 3 * NSA_HEADS,
            MLA_Q_RANK, MLA_KV_RANK, MLA_ROPE,
            SB_HEADS * SB_HD, 2 * SB_KV * SB_HD,
            N_BRANCH * D_MODEL)
N_IN = sum(IN_SIZES)

kernel_name = 'hybrid_nsa_mla_stickbreak_step'


def _rmsnorm(x, g):
    x32 = x.astype(jnp.float32)
    y = x32 * lax.rsqrt(jnp.mean(x32 * x32, axis=-1, keepdims=True) + EPS)
    return (y * g.astype(jnp.float32)).astype(x.dtype)


def _rope(x, pos):
    d = x.shape[-1]
    half = d // 2
    inv = jnp.exp(jnp.arange(half, dtype=jnp.float32) * (-2.0 * math.log(ROPE_THETA) / d))
    ang = pos[:, None] * inv[None, :]
    cos = jnp.cos(ang)[None, :, None, :]
    sin = jnp.sin(ang)[None, :, None, :]
    x32 = x.astype(jnp.float32)
    x1, x2 = x32[..., :half], x32[..., half:]
    return jnp.concatenate([x1 * cos - x2 * sin, x1 * sin + x2 * cos], axis=-1).astype(x.dtype)


def _rope_kv(kv, pos):
    return jnp.stack([_rope(kv[:, :, 0], pos), kv[:, :, 1]], axis=2)


def _extend(past, name, new):
    return new if past is None else jnp.concatenate([past[name], new], axis=1)


def _sweep(fn, q0, *qs):
    nq = qs[0].shape[1]
    qb = Q_BLOCK if nq % Q_BLOCK == 0 else nq
    nb = nq // qb
    starts = q0 + qb * jnp.arange(nb, dtype=jnp.int32)
    blocks = tuple(jnp.moveaxis(a.reshape(a.shape[0], nb, qb, *a.shape[2:]), 1, 0) for a in qs)
    out = lax.map(lambda xs: fn(xs[0], *xs[1]), (starts, blocks))
    return jnp.moveaxis(out, 0, 1).reshape(out.shape[1], nq, *out.shape[3:])


def _nsa(q, gates, kv_c, kv_s, kv_w, w_pos0, q0, w_cmp):
    B, T = kv_c.shape[:2]
    rep = NSA_HEADS // NSA_KV
    scale = NSA_HD ** -0.5
    nbc = T // NSA_BLOCK
    nbs = -(-T // NSA_BLOCK)
    n_sel = min(NSA_TOPN, nbs)
    cblk = kv_c[:, :nbc * NSA_BLOCK].reshape(B, nbc, NSA_BLOCK, 2, NSA_KV, NSA_HD)
    kc = jnp.einsum('bnlgd,ld->bngd', cblk[:, :, :, 0], w_cmp[0])
    vc = jnp.einsum('bnlgd,ld->bngd', cblk[:, :, :, 1], w_cmp[1])
    sblk = jnp.pad(kv_s, ((0, 0), (0, nbs * NSA_BLOCK - T), (0, 0), (0, 0), (0, 0)))
    sblk = sblk.reshape(B, nbs, NSA_BLOCK, 2, NSA_KV, NSA_HD)
    ks = jnp.transpose(sblk[:, :, :, 0], (0, 3, 1, 2, 4))
    vs = jnp.transpose(sblk[:, :, :, 1], (0, 3, 1, 2, 4))
    wpad = jnp.pad(kv_w, ((0, 0), (WINDOW, 0), (0, 0), (0, 0), (0, 0)))
    take_blocks = jax.vmap(jax.vmap(lambda blk, ix: blk[ix]))
    c_idx = jnp.arange(nbc)
    s_idx = jnp.arange(nbs)
    in_blk = jnp.arange(NSA_BLOCK)

    def block(t0, qblk, gblk):
        qb = qblk.shape[1]
        t = t0 + jnp.arange(qb)
        qg = qblk.reshape(B, qb, NSA_KV, rep, NSA_HD)
        sc = jnp.einsum('bqgrd,bngd->bgrqn', qg, kc).astype(jnp.float32) * scale
        cvalid = (c_idx[None, :] + 1) * NSA_BLOCK - 1 <= t[:, None]
        pc = jax.nn.softmax(jnp.where(cvalid, sc, NEG), axis=-1) * cvalid
        o_c = jnp.einsum('bgrqn,bngd->bqgrd', pc.astype(vc.dtype), vc)
        imp = jnp.pad(pc.sum(axis=2), ((0, 0), (0, 0), (0, 0), (0, nbs - nbc)))
        cur = (t // NSA_BLOCK)[:, None]
        j = s_idx[None, :]
        forced = (j == 0) | (j == cur) | (j == cur - 1)
        score = jnp.where(j > cur, -FORCE, jnp.where(forced, FORCE, imp))
        _, idx = lax.top_k(score, n_sel)
        ksel = take_blocks(ks, idx)
        vsel = take_blocks(vs, idx)
        kpos = idx[..., None] * NSA_BLOCK + in_blk
        svalid = (kpos <= t[None, None, :, None, None])[:, :, None]
        ss = jnp.einsum('bqgrd,bgqnkd->bgrqnk', qg, ksel).astype(jnp.float32) * scale
        ss = jnp.where(svalid, ss, NEG).reshape(B, NSA_KV, rep, qb, n_sel * NSA_BLOCK)
        ps = jax.nn.softmax(ss, axis=-1).reshape(B, NSA_KV, rep, qb, n_sel, NSA_BLOCK)
        o_s = jnp.einsum('bgrqnk,bgqnkd->bqgrd', ps.astype(vsel.dtype), vsel)
        band = lax.dynamic_slice_in_dim(wpad, t0 - w_pos0, qb + WINDOW, axis=1)
        spos = t0 - WINDOW + jnp.arange(qb + WINDOW)
        wvalid = (spos[None, :] >= w_pos0) & (spos[None, :] <= t[:, None]) & (spos[None, :] >= t[:, None] - WINDOW)
        sw = jnp.einsum('bqgrd,bsgd->bgrqs', qg, band[:, :, 0]).astype(jnp.float32) * scale
        pw = jax.nn.softmax(jnp.where(wvalid, sw, NEG), axis=-1)
        o_w = jnp.einsum('bgrqs,bsgd->bqgrd', pw.astype(band.dtype), band[:, :, 1])
        g = gblk.reshape(B, qb, NSA_KV, rep, 3).astype(o_c.dtype)
        o = o_c * g[..., 0:1] + o_s * g[..., 1:2] + o_w * g[..., 2:3]
        return o.reshape(B, qb, NSA_HEADS, NSA_HD)

    return _sweep(block, q0, q, gates)


def _mla(q_lat, q_rope, ckv, kr, q0):
    scale = (MLA_NOPE + MLA_ROPE) ** -0.5
    s_pos = jnp.arange(ckv.shape[1])

    def block(t0, ql, qr):
        t = t0 + jnp.arange(ql.shape[1])
        sc = (jnp.einsum('bqhc,bsc->bhqs', ql, ckv) + jnp.einsum('bqhr,bsr->bhqs', qr, kr)).astype(jnp.float32) * scale
        p = jax.nn.softmax(jnp.where(s_pos[None, :] <= t[:, None], sc, NEG), axis=-1)
        return jnp.einsum('bhqs,bsc->bqhc', p.astype(ckv.dtype), ckv)

    return _sweep(block, q0, q_lat, q_rope)


def _stick_breaking(q, kv, q0):
    B = q.shape[0]
    rep = SB_HEADS // SB_KV
    scale = SB_HD ** -0.5
    k, v = kv[:, :, 0], kv[:, :, 1]
    s_pos = jnp.arange(kv.shape[1])

    def block(t0, qblk):
        qb = qblk.shape[1]
        t = t0 + jnp.arange(qb)
        qg = qblk.reshape(B, qb, SB_KV, rep, SB_HD)
        z = jnp.einsum('bqgrd,bsgd->bgrqs', qg, k).astype(jnp.float32) * scale
        valid = s_pos[None, :] < t[:, None]
        log_rem = jnp.where(valid, jax.nn.log_sigmoid(-z), 0.0)
        suffix = lax.cumsum(log_rem, axis=z.ndim - 1, reverse=True)
        between = jnp.concatenate([suffix[..., 1:], jnp.zeros_like(suffix[..., :1])], axis=-1)
        a = jnp.where(valid, jnp.exp(jax.nn.log_sigmoid(z) + between), 0.0)
        o = jnp.einsum('bgrqs,bsgd->bqgrd', a.astype(v.dtype), v)
        return o.reshape(B, qb, SB_HEADS, SB_HD)

    return _sweep(block, q0, q)


def _mixer(h, lw, q0, past):
    B, NQ, _ = h.shape
    pos = q0 + jnp.arange(NQ, dtype=jnp.float32)
    cuts = [int(c) for c in np.cumsum(IN_SIZES)[:-1]]
    (a_q, a_cmp, a_sel, a_win, a_gate, b_cq, b_ckv, b_kr, c_q, c_kv, gate) = jnp.split(h @ lw['w_in'], cuts, axis=-1)
    kv_shape = (B, NQ, 2, NSA_KV, NSA_HD)
    qa = _rope(a_q.reshape(B, NQ, NSA_HEADS, NSA_HD), pos)
    kv_c = _rope_kv(a_cmp.reshape(kv_shape), pos)
    kv_s = _rope_kv(a_sel.reshape(kv_shape), pos)
    kv_w = _rope_kv(a_win.reshape(kv_shape), pos)
    ga = jax.nn.sigmoid(a_gate.reshape(B, NQ, NSA_HEADS, 3))
    full_w = _extend(past, 'win', kv_w)
    w_pos0 = q0 - (full_w.shape[1] - NQ)
    o_a = _nsa(qa, ga, _extend(past, 'cmp', kv_c), _extend(past, 'sel', kv_s), full_w, w_pos0, q0, lw['w_cmp'])
    new_win = full_w[:, full_w.shape[1] - min(WINDOW, full_w.shape[1]):]
    cq = _rmsnorm(b_cq, lw['norm_cq'])
    qm = (cq @ lw['w_uq']).reshape(B, NQ, MLA_HEADS, MLA_NOPE + MLA_ROPE)
    q_nope = qm[..., :MLA_NOPE]
    q_rope = _rope(qm[..., MLA_NOPE:], pos)
    ckv = _rmsnorm(b_ckv, lw['norm_ckv'])
    kr = _rope(b_kr[:, :, None, :], pos)[:, :, 0]
    q_lat = jnp.einsum('bqhd,chd->bqhc', q_nope, lw['w_uk'])
    o_lat = _mla(q_lat, q_rope, _extend(past, 'ckv', ckv), _extend(past, 'kr', kr), q0)
    o_b = jnp.einsum('bqhc,chv->bqhv', o_lat, lw['w_uv'])
    qc = c_q.reshape(B, NQ, SB_HEADS, SB_HD)
    kv_sb = c_kv.reshape(B, NQ, 2, SB_KV, SB_HD)
    o_c = _stick_breaking(qc, _extend(past, 'sb', kv_sb), q0)
    branches = jnp.stack([o_a.reshape(B, NQ, BRANCH_W), o_b.reshape(B, NQ, BRANCH_W), o_c.reshape(B, NQ, BRANCH_W)], axis=2)
    proj = jnp.einsum('bqiw,iwd->bqid', branches, lw['w_branch'])
    g = jax.nn.sigmoid(gate.astype(jnp.float32)).reshape(B, NQ, N_BRANCH, D_MODEL)
    merged = jnp.sum(g * proj.astype(jnp.float32), axis=2).astype(h.dtype)
    state = dict(cmp=kv_c, sel=kv_s, win=new_win, ckv=ckv, kr=kr, sb=kv_sb)
    return merged @ lw['w_out'], state


def _mem_kv(mem, lw):
    B, M, _ = mem.shape
    return (_rmsnorm(mem, lw['norm_mem']) @ lw['w_xkv']).reshape(B, M, 2, X_HEADS, X_HD)


def _cross(h, mem_kv, lw):
    B, NQ, _ = h.shape
    q = (h @ lw['w_xq']).reshape(B, NQ, X_HEADS, X_HD)
    s = jnp.einsum('bqhd,bmhd->bhqm', q, mem_kv[:, :, 0]).astype(jnp.float32) * X_HD ** -0.5
    p = jax.nn.softmax(s, axis=-1).astype(mem_kv.dtype)
    o = jnp.einsum('bhqm,bmhd->bqhd', p, mem_kv[:, :, 1]).reshape(B, NQ, X_HEADS * X_HD)
    return o @ lw['w_xo']


def _conv_ffn(h, lw, prev):
    B, NQ, _ = h.shape
    u = h @ lw['w_up']
    if prev is None:
        prev = jnp.zeros((B, CONV_W - 1, u.shape[-1]), u.dtype)
    ext = jnp.concatenate([prev, u], axis=1)
    c = lw['conv_b'] + sum(ext[:, i:i + NQ] * lw['conv_w'][i] for i in range(CONV_W))
    a, b = jnp.split(c, 2, axis=-1)
    return (jax.nn.silu(a) * b) @ lw['w_down'], ext[:, ext.shape[1] - (CONV_W - 1):]


def _layer(x, lw, q0, past, mem_kv):
    mix, state = _mixer(_rmsnorm(x, lw['norm_mix']), lw, q0, past)
    x = x + mix
    x = x + _cross(_rmsnorm(x, lw['norm_cross']), mem_kv, lw)
    f, conv_state = _conv_ffn(_rmsnorm(x, lw['norm_ffn']), lw, None if past is None else past['conv'])
    state['conv'] = conv_state
    return x + f, state


def setup_inputs(seed: int = 0) -> dict:
    key = jax.random.key(seed)
    ks = iter(jax.random.split(key, 48))
    nrm = lambda shape, scale=1.0: jax.random.normal(next(ks), shape, jnp.float32) * scale
    gain = lambda shape: 1.0 + 0.05 * jax.random.normal(next(ks), shape, jnp.float32)
    n_pages = PAST_LEN // PAGE_SIZE
    n_used = DEC_BATCH * n_pages
    n_pool = n_used + max(1, n_used // 4)
    wb = min(WINDOW, PAST_LEN)
    L = DEPTH
    return {
        'x_prompt': nrm((BATCH, SEQ, D_MODEL)),
        'x_sample': nrm((DEC_BATCH, DEC_SEQ, D_MODEL)),
        'cache_nsa_cmp': nrm((L, n_pool, PAGE_SIZE, 2, NSA_KV, NSA_HD)),
        'cache_nsa_sel': nrm((L, n_pool, PAGE_SIZE, 2, NSA_KV, NSA_HD)),
        'state_nsa_win': nrm((L, DEC_BATCH, wb, 2, NSA_KV, NSA_HD)),
        'cache_mla_ckv': nrm((L, n_pool, PAGE_SIZE, MLA_KV_RANK)),
        'cache_mla_krope': nrm((L, n_pool, PAGE_SIZE, MLA_ROPE)),
        'cache_sb_kv': nrm((L, n_pool, PAGE_SIZE, 2, SB_KV, SB_HD)),
        'cache_mem_kv': nrm((L, DEC_BATCH, N_MEM, 2, X_HEADS, X_HD)),
        'state_conv': nrm((L, DEC_BATCH, CONV_W - 1, 2 * D_FF)),
        'page_table': jax.random.permutation(next(ks), n_pool)[:n_used].astype(jnp.int32).reshape(DEC_BATCH, n_pages),
        'mem_prompt': nrm((BATCH, N_MEM, D_MODEL)),
        'norm_mix': gain((L, D_MODEL)),
        'w_in': nrm((L, D_MODEL, N_IN), D_MODEL ** -0.5),
        'norm_cq': gain((L, MLA_Q_RANK)),
        'w_uq': nrm((L, MLA_Q_RANK, MLA_HEADS * (MLA_NOPE + MLA_ROPE)), MLA_Q_RANK ** -0.5),
        'norm_ckv': gain((L, MLA_KV_RANK)),
        'w_uk': nrm((L, MLA_KV_RANK, MLA_HEADS, MLA_NOPE), MLA_KV_RANK ** -0.5),
        'w_uv': nrm((L, MLA_KV_RANK, MLA_HEADS, MLA_V), MLA_KV_RANK ** -0.5),
        'w_cmp': nrm((L, 2, NSA_BLOCK, NSA_HD), NSA_BLOCK ** -0.5),
        'w_branch': nrm((L, N_BRANCH, BRANCH_W, D_MODEL), BRANCH_W ** -0.5),
        'w_out': nrm((L, D_MODEL, D_MODEL), D_MODEL ** -0.5),
        'norm_cross': gain((L, D_MODEL)),
        'norm_mem': gain((L, D_MODEL)),
        'w_xq': nrm((L, D_MODEL, X_HEADS * X_HD), D_MODEL ** -0.5),
        'w_xkv': nrm((L, D_MODEL, 2 * X_HEADS * X_HD), D_MODEL ** -0.5),
        'w_xo': nrm((L, X_HEADS * X_HD, D_MODEL), (X_HEADS * X_HD) ** -0.5),
        'norm_ffn': gain((L, D_MODEL)),
        'w_up': nrm((L, D_MODEL, 2 * D_FF), D_MODEL ** -0.5),
        'conv_w': nrm((L, CONV_W, 2 * D_FF), CONV_W ** -0.5),
        'conv_b': nrm((L, 2 * D_FF), 0.01),
        'w_down': nrm((L, D_FF, D_MODEL), D_FF ** -0.5),
        'norm_final': gain((D_MODEL,)),
    }


def reference(x_prompt, x_sample, cache_nsa_cmp, cache_nsa_sel, state_nsa_win, cache_mla_ckv, cache_mla_krope,
              cache_sb_kv, cache_mem_kv, state_conv, page_table, mem_prompt, norm_mix, w_in, norm_cq, w_uq,
              norm_ckv, w_uk, w_uv, w_cmp, w_branch, w_out, norm_cross, norm_mem, w_xq, w_xkv, w_xo,
              norm_ffn, w_up, conv_w, conv_b, w_down, norm_final):
    n_seq, n_pages = page_table.shape
    past_len = n_pages * cache_nsa_cmp.shape[2]

    def gather(pool):
        return pool[page_table].reshape(n_seq, past_len, *pool.shape[2:])

    def stack(states, name):
        return jnp.stack([s[name] for s in states])

    xp, xs = x_prompt, x_sample
    st_p, st_s = [], []
    for l in range(DEPTH):
        lw = dict(norm_mix=norm_mix[l], w_in=w_in[l], norm_cq=norm_cq[l], w_uq=w_uq[l], norm_ckv=norm_ckv[l],
                  w_uk=w_uk[l], w_uv=w_uv[l], w_cmp=w_cmp[l], w_branch=w_branch[l], w_out=w_out[l],
                  norm_cross=norm_cross[l], norm_mem=norm_mem[l], w_xq=w_xq[l], w_xkv=w_xkv[l], w_xo=w_xo[l],
                  norm_ffn=norm_ffn[l], w_up=w_up[l], conv_w=conv_w[l], conv_b=conv_b[l], w_down=w_down[l])
        mem_kv = _mem_kv(mem_prompt, lw)
        xp, sp = _layer(xp, lw, 0, None, mem_kv)
        sp['mem'] = mem_kv
        past = dict(cmp=gather(cache_nsa_cmp[l]), sel=gather(cache_nsa_sel[l]), win=state_nsa_win[l],
                    ckv=gather(cache_mla_ckv[l]), kr=gather(cache_mla_krope[l]), sb=gather(cache_sb_kv[l]),
                    conv=state_conv[l])
        xs, ss = _layer(xs, lw, past_len, past, cache_mem_kv[l])
        st_p.append(sp)
        st_s.append(ss)
    y_prompt = _rmsnorm(xp, norm_final)
    y_sample = _rmsnorm(xs, norm_final)
    nsa_cmp_p, nsa_cmp_s = stack(st_p, 'cmp'), stack(st_s, 'cmp')
    nsa_sel_p, nsa_sel_s = stack(st_p, 'sel'), stack(st_s, 'sel')
    nsa_win_p, nsa_win_s = stack(st_p, 'win'), stack(st_s, 'win')
    mla_ckv_p, mla_ckv_s = stack(st_p, 'ckv'), stack(st_s, 'ckv')
    mla_kr_p, mla_kr_s = stack(st_p, 'kr'), stack(st_s, 'kr')
    sb_p, sb_s = stack(st_p, 'sb'), stack(st_s, 'sb')
    mem_p = stack(st_p, 'mem')
    conv_p, conv_s = stack(st_p, 'conv'), stack(st_s, 'conv')
    return (y_prompt, y_sample, nsa_cmp_p, nsa_cmp_s, nsa_sel_p, nsa_sel_s, nsa_win_p, nsa_win_s,
            mla_ckv_p, mla_ckv_s, mla_kr_p, mla_kr_s, sb_p, sb_s, mem_p, conv_p, conv_s)
```

```python
import functools
import math

import numpy as np
import jax
import jax.numpy as jnp
from jax import lax
from jax.experimental import pallas as pl
from jax.experimental.pallas import tpu as pltpu

F32 = jnp.float32
BF16 = jnp.bfloat16

NSA_HEADS, NSA_KV, NSA_HD, NSA_BLOCK, NSA_TOPN, WINDOW = 8, 2, 64, 64, 16, 512
NSA_REP = NSA_HEADS // NSA_KV
MLA_HEADS, MLA_Q_RANK, MLA_KV_RANK, MLA_NOPE, MLA_ROPE, MLA_V = 8, 384, 256, 64, 32, 64
SB_HEADS, SB_KV, SB_HD = 8, 4, 64
SB_REP = SB_HEADS // SB_KV
N_BRANCH, BRANCH_W = 3, 512
X_HEADS, X_HD = 4, 128
CONV_W = 3
ROPE_THETA = 10000.0
EPS = 1e-6
NEG = -1e30
FORCE = 1e9

LANES = 128
SUBLANES = 8
VMEM_LIMIT_CAP = 60 * 1024 * 1024

_IN_SIZES = (NSA_HEADS * NSA_HD, 2 * NSA_KV * NSA_HD, 2 * NSA_KV * NSA_HD, 2 * NSA_KV * NSA_HD, 3 * NSA_HEADS,
             MLA_Q_RANK, MLA_KV_RANK, MLA_ROPE, SB_HEADS * SB_HD, 2 * SB_KV * SB_HD)


def _round_up(n, m):
    return -(-n // m) * m


def _in_layout(d_model):
    sizes = _IN_SIZES + (N_BRANCH * d_model,)
    src, dst, s, d = [], [], 0, 0
    for sz in sizes:
        src.append(s)
        dst.append(d)
        s += sz
        d += _round_up(sz, LANES)
    return sizes, src, dst, d


def _pick_tile(n, cap, mult):
    best = None
    for t in range(mult, min(n, cap) + 1, mult):
        if n % t == 0:
            best = t
    return n if best is None else best


def _cparams(sem, vmem_bytes):
    limit = int(min(VMEM_LIMIT_CAP, max(32 * 1024 * 1024, vmem_bytes)))
    return pltpu.CompilerParams(dimension_semantics=sem, vmem_limit_bytes=limit)


def _rms(x, g):
    return x * lax.rsqrt(jnp.mean(x * x, axis=-1, keepdims=True) + EPS) * g


def _linear_kernel(*refs, norm, residual):
    x_ref = refs[0]
    pos = 1
    g_ref = None
    if norm:
        g_ref = refs[pos]
        pos += 1
    w_ref = refs[pos]
    pos += 1
    r_ref = None
    if residual:
        r_ref = refs[pos]
        pos += 1
    o_ref, xn_ref = refs[pos], refs[pos + 1]

    @pl.when(pl.program_id(1) == 0)
    def _():
        x = x_ref[...]
        if norm:
            x = _rms(x, g_ref[...])
        xn_ref[...] = x.astype(BF16)

    acc = jnp.dot(xn_ref[...], w_ref[...], preferred_element_type=F32)
    if residual:
        acc = acc + r_ref[...]
    o_ref[...] = acc


def _linear(x, w, gain=None, residual=None, name="linear"):
    M, K = x.shape
    N = w.shape[1]
    tm = _pick_tile(M, 512, SUBLANES)
    tn = _pick_tile(N, 1024, LANES)
    args, specs = [x], [pl.BlockSpec((tm, K), lambda i, j: (i, 0))]
    if gain is not None:
        args.append(gain.reshape(1, K))
        specs.append(pl.BlockSpec((1, K), lambda i, j: (0, 0)))
    args.append(w)
    specs.append(pl.BlockSpec((K, tn), lambda i, j: (0, j)))
    if residual is not None:
        args.append(residual)
        specs.append(pl.BlockSpec((tm, tn), lambda i, j: (i, j)))
    vmem = 2 * (tm * K * 4 + K * tn * 2 + 2 * tm * tn * 4) + tm * K * 2 + 2 * tm * tn * 4
    return pl.pallas_call(
        functools.partial(_linear_kernel, norm=gain is not None, residual=residual is not None),
        grid=(M // tm, N // tn),
        in_specs=specs,
        out_specs=pl.BlockSpec((tm, tn), lambda i, j: (i, j)),
        out_shape=jax.ShapeDtypeStruct((M, N), F32),
        scratch_shapes=[pltpu.VMEM((tm, K), BF16)],
        compiler_params=_cparams(("parallel", "arbitrary"), vmem + (8 << 20)),
        name=name,
    )(*args)


def _rmsnorm_kernel(x_ref, g_ref, o_ref):
    o_ref[...] = _rms(x_ref[...], g_ref[...])


def _rmsnorm(x, gain, name="rmsnorm"):
    M, K = x.shape
    tm = _pick_tile(M, 1024, SUBLANES)
    return pl.pallas_call(
        _rmsnorm_kernel,
        grid=(M // tm,),
        in_specs=[pl.BlockSpec((tm, K), lambda i: (i, 0)), pl.BlockSpec((1, K), lambda i: (0, 0))],
        out_specs=pl.BlockSpec((tm, K), lambda i: (i, 0)),
        out_shape=jax.ShapeDtypeStruct((M, K), F32),
        compiler_params=_cparams(("parallel",), 4 * tm * K * 4 + (8 << 20)),
        name=name,
    )(x, gain.reshape(1, K))


def _bmm_kernel(x_ref, w_ref, o_ref):
    o_ref[...] = jnp.dot(x_ref[...].astype(BF16), w_ref[...], preferred_element_type=F32)


def _bmm(x, w, name="bmm"):
    H, M, din = x.shape
    dout = w.shape[2]
    tm = _pick_tile(M, 1024, SUBLANES)
    return pl.pallas_call(
        _bmm_kernel,
        grid=(H, M // tm),
        in_specs=[pl.BlockSpec((None, tm, din), lambda h, i: (h, i, 0)),
                  pl.BlockSpec((None, din, dout), lambda h, i: (h, 0, 0))],
        out_specs=pl.BlockSpec((None, tm, dout), lambda h, i: (h, i, 0)),
        out_shape=jax.ShapeDtypeStruct((H, M, dout), F32),
        compiler_params=_cparams(("parallel", "parallel"), 32 << 20),
        name=name,
    )(x, w)


def _merge_kernel(oa_ref, ob_ref, oc_ref, gate_ref, x_ref, wb_ref, wo_ref, o_ref, *, d_model):
    merged = None
    for i, o_r in enumerate((oa_ref, ob_ref, oc_ref)):
        proj = jnp.dot(o_r[...].astype(BF16), wb_ref[i], preferred_element_type=F32)
        term = jax.nn.sigmoid(gate_ref[:, i * d_model:(i + 1) * d_model]) * proj
        merged = term if merged is None else merged + term
    o_ref[...] = x_ref[...] + jnp.dot(merged.astype(BF16), wo_ref[...], preferred_element_type=F32)


def _merge(oa, ob, oc, gate, x, wb, wo):
    M, D = x.shape
    tm = _pick_tile(M, 256, SUBLANES)
    row = lambda w: pl.BlockSpec((tm, w), lambda i: (i, 0))
    vmem = 2 * (3 * tm * BRANCH_W * 4 + tm * 3 * D * 4 + 2 * tm * D * 4 + 3 * BRANCH_W * D * 2 + D * D * 2) + 4 * tm * D * 4
    return pl.pallas_call(
        functools.partial(_merge_kernel, d_model=D),
        grid=(M // tm,),
        in_specs=[row(BRANCH_W), row(BRANCH_W), row(BRANCH_W), row(N_BRANCH * D), row(D),
                  pl.BlockSpec((N_BRANCH, BRANCH_W, D), lambda i: (0, 0, 0)),
                  pl.BlockSpec((D, D), lambda i: (0, 0))],
        out_specs=row(D),
        out_shape=jax.ShapeDtypeStruct((M, D), F32),
        compiler_params=_cparams(("parallel",), vmem + (8 << 20)),
        name="merge",
    )(oa, ob, oc, gate, x, wb, wo)


def _conv_gate(u, p1, p2, w, b):
    return b + p2 * w[0:1] + p1 * w[1:2] + u * w[2:3]


def _convgate_prompt_kernel(ua_ref, ub_ref, ha_ref, hb_ref, wa_ref, wb_ref, ba_ref, bb_ref, o_ref, *, tiles_per_seq):
    first = (pl.program_id(0) % tiles_per_seq) == 0
    tm = ua_ref.shape[0]

    def conv(u_ref, h_ref, w_ref, b_ref):
        u = u_ref[...]
        halo = jnp.where(first, 0.0, h_ref[...])
        ext = jnp.concatenate([halo, u], axis=0)
        return _conv_gate(u, ext[SUBLANES - 1:SUBLANES - 1 + tm], ext[SUBLANES - 2:SUBLANES - 2 + tm], w_ref[...], b_ref[...])

    a = conv(ua_ref, ha_ref, wa_ref, ba_ref)
    b = conv(ub_ref, hb_ref, wb_ref, bb_ref)
    o_ref[...] = a * jax.nn.sigmoid(a) * b


def _convgate_prompt(u, conv_w, conv_b, seq):
    M, F2 = u.shape
    Fh = F2 // 2
    tm = _pick_tile(seq, 512, SUBLANES)
    tn = _pick_tile(Fh, 512, LANES)
    nj = Fh // tn
    hb = tm // SUBLANES
    halo = lambda off: pl.BlockSpec((SUBLANES, tn), lambda i, j: (jnp.maximum(i * hb - 1, 0), j + off))
    return pl.pallas_call(
        functools.partial(_convgate_prompt_kernel, tiles_per_seq=seq // tm),
        grid=(M // tm, nj),
        in_specs=[pl.BlockSpec((tm, tn), lambda i, j: (i, j)), pl.BlockSpec((tm, tn), lambda i, j: (i, j + nj)),
                  halo(0), halo(nj),
                  pl.BlockSpec((CONV_W, tn), lambda i, j: (0, j)), pl.BlockSpec((CONV_W, tn), lambda i, j: (0, j + nj)),
                  pl.BlockSpec((1, tn), lambda i, j: (0, j)), pl.BlockSpec((1, tn), lambda i, j: (0, j + nj))],
        out_specs=pl.BlockSpec((tm, tn), lambda i, j: (i, j)),
        out_shape=jax.ShapeDtypeStruct((M, Fh), F32),
        compiler_params=_cparams(("parallel", "parallel"), 32 << 20),
        name="convgate_prompt",
    )(u, u, u, u, conv_w, conv_w, conv_b.reshape(1, F2), conv_b.reshape(1, F2))


def _convgate_decode_kernel(ua_ref, ub_ref, p2a_ref, p2b_ref, p1a_ref, p1b_ref, wa_ref, wb_ref, ba_ref, bb_ref, o_ref):
    a = _conv_gate(ua_ref[...], p1a_ref[...], p2a_ref[...], wa_ref[...], ba_ref[...])
    b = _conv_gate(ub_ref[...], p1b_ref[...], p2b_ref[...], wb_ref[...], bb_ref[...])
    o_ref[...] = a * jax.nn.sigmoid(a) * b


def _convgate_decode(u, state2d, conv_w, conv_b):
    M, F2 = u.shape
    Fh = F2 // 2
    tn = _pick_tile(Fh, 512, LANES)
    nj = Fh // tn
    col = lambda off: pl.BlockSpec((M, tn), lambda j: (0, j + off))
    return pl.pallas_call(
        _convgate_decode_kernel,
        grid=(nj,),
        in_specs=[col(0), col(nj), col(0), col(nj), col(2 * nj), col(3 * nj),
                  pl.BlockSpec((CONV_W, tn), lambda j: (0, j)), pl.BlockSpec((CONV_W, tn), lambda j: (0, j + nj)),
                  pl.BlockSpec((1, tn), lambda j: (0, j)), pl.BlockSpec((1, tn), lambda j: (0, j + nj))],
        out_specs=pl.BlockSpec((M, tn), lambda j: (0, j)),
        out_shape=jax.ShapeDtypeStruct((M, Fh), F32),
        compiler_params=_cparams(("parallel",), 32 << 20),
        name="convgate_decode",
    )(u, u, state2d, state2d, state2d, state2d, conv_w, conv_w, conv_b.reshape(1, F2), conv_b.reshape(1, F2))


def _cross_prompt_kernel(x_ref, g_ref, wq_ref, kT_ref, v_ref, wo_ref, o_ref):
    x = x_ref[...]
    xn = _rms(x, g_ref[...]).astype(BF16)
    q = jnp.dot(xn, wq_ref[...], preferred_element_type=F32).astype(BF16)
    scale = X_HD ** -0.5
    outs = []
    for h in range(X_HEADS):
        s = jnp.dot(q[:, h * X_HD:(h + 1) * X_HD], kT_ref[h], preferred_element_type=F32) * scale
        m = s.max(-1, keepdims=True)
        e = jnp.exp(s - m)
        p = e / e.sum(-1, keepdims=True)
        outs.append(jnp.dot(p.astype(BF16), v_ref[h], preferred_element_type=F32))
    o = jnp.concatenate(outs, axis=-1).astype(BF16)
    o_ref[...] = x + jnp.dot(o, wo_ref[...], preferred_element_type=F32)


def _cross_prompt(x, gain, wq, kT, v, wo):
    B, S, D = x.shape
    Nm = v.shape[2]
    tm = _pick_tile(S, 512, SUBLANES)
    HX = X_HEADS * X_HD
    return pl.pallas_call(
        _cross_prompt_kernel,
        grid=(B, S // tm),
        in_specs=[pl.BlockSpec((None, tm, D), lambda b, i: (b, i, 0)),
                  pl.BlockSpec((1, D), lambda b, i: (0, 0)),
                  pl.BlockSpec((D, HX), lambda b, i: (0, 0)),
                  pl.BlockSpec((None, X_HEADS, X_HD, Nm), lambda b, i: (b, 0, 0, 0)),
                  pl.BlockSpec((None, X_HEADS, Nm, X_HD), lambda b, i: (b, 0, 0, 0)),
                  pl.BlockSpec((HX, D), lambda b, i: (0, 0))],
        out_specs=pl.BlockSpec((None, tm, D), lambda b, i: (b, i, 0)),
        out_shape=jax.ShapeDtypeStruct((B, S, D), F32),
        compiler_params=_cparams(("parallel", "parallel"), 40 << 20),
        name="cross_prompt",
    )(x, gain.reshape(1, D), wq, kT, v, wo)


def _cross_decode_kernel(q_ref, mem_ref, o_ref):
    q = q_ref[...].astype(BF16)
    mem = mem_ref[...].astype(BF16)
    s = lax.dot_general(q, mem, (((1,), (1,)), ((), ())), preferred_element_type=F32) * (X_HD ** -0.5)
    col = lax.broadcasted_iota(jnp.int32, s.shape, 1)
    head = lax.broadcasted_iota(jnp.int32, s.shape, 0)
    mask = jnp.where((col % (2 * X_HEADS)) == head, 1.0, 0.0)
    s = s + (1.0 - mask) * NEG
    m = s.max(-1, keepdims=True)
    e = jnp.exp(s - m) * mask
    p = e / e.sum(-1, keepdims=True)
    pv = pltpu.roll(p, X_HEADS, axis=1)
    o_ref[...] = jnp.dot(pv.astype(BF16), mem, preferred_element_type=F32)


def _cross_decode(q, mem, l):
    NB = q.shape[0]
    R = mem.shape[2]
    H2 = 2 * X_HEADS
    qp = jnp.pad(q, ((0, 0), (0, X_HEADS), (0, 0)))
    out = pl.pallas_call(
        _cross_decode_kernel,
        grid=(NB,),
        in_specs=[pl.BlockSpec((None, H2, X_HD), lambda b: (b, 0, 0)),
                  pl.BlockSpec((None, None, R, X_HD), lambda b: (l, b, 0, 0))],
        out_specs=pl.BlockSpec((None, H2, X_HD), lambda b: (b, 0, 0)),
        out_shape=jax.ShapeDtypeStruct((NB, H2, X_HD), F32),
        compiler_params=_cparams(("parallel",), 32 << 20),
        name="cross_decode",
    )(qp, mem)
    return out[:, :X_HEADS]


def _nsa_compress_kernel(x_ref, w_ref, o_ref):
    x = x_ref[...]
    nb = x.shape[0] // NSA_BLOCK
    o_ref[...] = (x.reshape(nb, NSA_BLOCK, x.shape[1]) * w_ref[...][None]).sum(axis=1)


def _nsa_compress(kv, wfull):
    B, S, C = kv.shape
    ts = _pick_tile(S, 512, NSA_BLOCK * SUBLANES) if S % (NSA_BLOCK * SUBLANES) == 0 else S
    return pl.pallas_call(
        _nsa_compress_kernel,
        grid=(B, S // ts),
        in_specs=[pl.BlockSpec((None, ts, C), lambda b, i: (b, i, 0)), pl.BlockSpec((NSA_BLOCK, C), lambda b, i: (0, 0))],
        out_specs=pl.BlockSpec((None, ts // NSA_BLOCK, C), lambda b, i: (b, i, 0)),
        out_shape=jax.ShapeDtypeStruct((B, S // NSA_BLOCK, C), F32),
        compiler_params=_cparams(("parallel", "parallel"), 32 << 20),
        name="nsa_compress",
    )(kv, wfull)


def _nsa_prompt_kernel(q_ref, kcT_ref, vc_ref, ksT_ref, vs_ref, kwT_ref, vw_ref, g_ref, o_ref,
                       m_scr, l_scr, acc_scr, *, tq, kb, n_sel):
    R = NSA_REP
    t0 = pl.program_id(2) * tq
    NBK = kcT_ref.shape[1]
    q = q_ref[...].reshape(R * tq, NSA_HD)
    ti = lax.broadcasted_iota(jnp.int32, (tq, 1), 0) + t0

    n_idx = lax.broadcasted_iota(jnp.int32, (1, NBK), 1)
    cmask = jnp.where(((n_idx + 1) * NSA_BLOCK - 1) <= ti, 1.0, 0.0)
    sc = jnp.dot(q, kcT_ref[...], preferred_element_type=F32).reshape(R, tq, NBK)
    sc = sc + ((1.0 - cmask) * NEG)[None]
    mc = sc.max(-1, keepdims=True)
    ec = jnp.exp(sc - mc) * cmask[None]
    lc = ec.sum(-1, keepdims=True)
    pc = ec / jnp.where(lc > 0.0, lc, 1.0)
    o_c = jnp.dot(pc.reshape(R * tq, NBK).astype(BF16), vc_ref[...], preferred_element_type=F32)
    imp = pc[0]
    for r in range(1, R):
        imp = imp + pc[r]

    jf = n_idx.astype(F32)
    cur = ti // NSA_BLOCK
    forced = (n_idx == 0) | (n_idx == cur) | (n_idx == cur - 1)
    score = jnp.where(n_idx > cur, -jnp.inf, jnp.where(forced, FORCE, imp))
    sel = jnp.zeros((tq, NBK), F32)
    for _ in range(n_sel):
        mx = score.max(-1, keepdims=True)
        jmin = jnp.where(score == mx, jf, float(NBK)).min(-1, keepdims=True)
        hit = jf == jmin
        sel = jnp.where(hit, 1.0, sel)
        score = jnp.where(hit, -jnp.inf, score)
    sel_b = sel.astype(BF16)

    m_scr[...] = jnp.full(m_scr.shape, NEG, F32)
    l_scr[...] = jnp.zeros(l_scr.shape, F32)
    acc_scr[...] = jnp.zeros(acc_scr.shape, F32)
    bpc = kb // NSA_BLOCK

    def body(c, carry):
        k0 = pl.multiple_of(c * kb, kb)
        s = jnp.dot(q, ksT_ref[:, pl.ds(k0, kb)], preferred_element_type=F32).reshape(R, tq, kb)
        kpos = k0 + lax.broadcasted_iota(jnp.int32, (1, kb), 1)
        blk = c * bpc + lax.broadcasted_iota(jnp.int32, (NBK, kb), 1) // NSA_BLOCK
        expand = (lax.broadcasted_iota(jnp.int32, (NBK, kb), 0) == blk).astype(BF16)
        selk = jnp.dot(sel_b, expand, preferred_element_type=F32)
        mask = jnp.where(kpos <= ti, selk, 0.0)
        s = s + ((1.0 - mask) * NEG)[None]
        m_prev = m_scr[...]
        m_new = jnp.maximum(m_prev, s.max(-1, keepdims=True))
        alpha = jnp.exp(m_prev - m_new)
        p = jnp.exp(s - m_new) * mask[None]
        l_scr[...] = alpha * l_scr[...] + p.sum(-1, keepdims=True)
        m_scr[...] = m_new
        pv = jnp.dot(p.reshape(R * tq, kb).astype(BF16), vs_ref[pl.ds(k0, kb), :], preferred_element_type=F32)
        acc_scr[...] = alpha.reshape(R * tq, 1) * acc_scr[...] + pv
        return carry

    lax.fori_loop(0, (t0 + tq - 1) // kb + 1, body, 0)
    o_s = acc_scr[...] / l_scr[...].reshape(R * tq, 1)

    wk = kwT_ref.shape[1] if kwT_ref.shape[1] < WINDOW + tq else WINDOW + tq
    w0 = pl.multiple_of(jnp.maximum(t0 + tq - wk, 0), LANES)
    sw = jnp.dot(q, kwT_ref[:, pl.ds(w0, wk)], preferred_element_type=F32).reshape(R, tq, wk)
    kposw = w0 + lax.broadcasted_iota(jnp.int32, (1, wk), 1)
    wmask = jnp.where(kposw <= ti, 1.0, 0.0) * jnp.where(kposw >= ti - WINDOW, 1.0, 0.0)
    sw = sw + ((1.0 - wmask) * NEG)[None]
    mw = sw.max(-1, keepdims=True)
    ew = jnp.exp(sw - mw) * wmask[None]
    pw = ew / ew.sum(-1, keepdims=True)
    o_w = jnp.dot(pw.reshape(R * tq, wk).astype(BF16), vw_ref[pl.ds(w0, wk), :], preferred_element_type=F32)

    g = g_ref[...]
    for r in range(R):
        rows = slice(r * tq, (r + 1) * tq)
        o_ref[r] = (o_c[rows] * g[:, 3 * r:3 * r + 1] + o_s[rows] * g[:, 3 * r + 1:3 * r + 2]
                    + o_w[rows] * g[:, 3 * r + 2:3 * r + 3])


def _nsa_prompt(q, kcT, vc, ksT, vs, kwT, vw, gates):
    B, G, R, S, hd = q.shape
    NBK = kcT.shape[3]
    tq = _pick_tile(S, 128, SUBLANES)
    kb = _pick_tile(S, 512, LANES)
    n_sel = min(NSA_TOPN, -(-S // NSA_BLOCK))
    per_bg = lambda shp: pl.BlockSpec((None, None) + shp, lambda b, g, i: (b, g) + (0,) * len(shp))
    return pl.pallas_call(
        functools.partial(_nsa_prompt_kernel, tq=tq, kb=kb, n_sel=n_sel),
        grid=(B, G, S // tq),
        in_specs=[pl.BlockSpec((None, None, R, tq, hd), lambda b, g, i: (b, g, 0, i, 0)),
                  per_bg((hd, NBK)), per_bg((NBK, hd)), per_bg((hd, S)), per_bg((S, hd)), per_bg((hd, S)), per_bg((S, hd)),
                  pl.BlockSpec((None, None, tq, 3 * R), lambda b, g, i: (b, g, i, 0))],
        out_specs=pl.BlockSpec((None, None, R, tq, hd), lambda b, g, i: (b, g, 0, i, 0)),
        out_shape=jax.ShapeDtypeStruct((B, G, R, S, hd), F32),
        scratch_shapes=[pltpu.VMEM((R, tq, 1), F32), pltpu.VMEM((R, tq, 1), F32), pltpu.VMEM((R * tq, hd), F32)],
        compiler_params=_cparams(("parallel", "parallel", "arbitrary"), 48 << 20),
        name="nsa_prompt",
    )(q, kcT, vc, ksT, vs, kwT, vw, gates)


def _mla_prompt_kernel(ql_ref, qr_ref, ckvT_ref, krT_ref, ckv_ref, wuv_ref, o_ref, m_scr, l_scr, acc_scr, *, tq, kb):
    H = MLA_HEADS
    t0 = pl.program_id(1) * tq
    ql = ql_ref[...].reshape(H * tq, MLA_KV_RANK)
    qr = qr_ref[...].reshape(H * tq, MLA_ROPE)
    ti = lax.broadcasted_iota(jnp.int32, (tq, 1), 0) + t0
    scale = (MLA_NOPE + MLA_ROPE) ** -0.5
    m_scr[...] = jnp.full(m_scr.shape, NEG, F32)
    l_scr[...] = jnp.zeros(l_scr.shape, F32)
    acc_scr[...] = jnp.zeros(acc_scr.shape, F32)

    def body(c, carry):
        k0 = pl.multiple_of(c * kb, kb)
        s = (jnp.dot(ql, ckvT_ref[:, pl.ds(k0, kb)], preferred_element_type=F32)
             + jnp.dot(qr, krT_ref[:, pl.ds(k0, kb)], preferred_element_type=F32)) * scale
        s = s.reshape(H, tq, kb)
        kpos = k0 + lax.broadcasted_iota(jnp.int32, (1, kb), 1)
        mask = jnp.where(kpos <= ti, 1.0, 0.0)
        s = s + ((1.0 - mask) * NEG)[None]
        m_prev = m_scr[...]
        m_new = jnp.maximum(m_prev, s.max(-1, keepdims=True))
        alpha = jnp.exp(m_prev - m_new)
        p = jnp.exp(s - m_new) * mask[None]
        l_scr[...] = alpha * l_scr[...] + p.sum(-1, keepdims=True)
        m_scr[...] = m_new
        pv = jnp.dot(p.reshape(H * tq, kb).astype(BF16), ckv_ref[pl.ds(k0, kb), :], preferred_element_type=F32)
        acc_scr[...] = alpha.reshape(H * tq, 1) * acc_scr[...] + pv
        return carry

    lax.fori_loop(0, (t0 + tq - 1) // kb + 1, body, 0)
    o_lat = (acc_scr[...] / l_scr[...].reshape(H * tq, 1)).astype(BF16)
    for h in range(H):
        o_ref[h] = jnp.dot(o_lat[h * tq:(h + 1) * tq], wuv_ref[h], preferred_element_type=F32)


def _mla_prompt(ql, qr, ckvT, krT, ckv, wuv):
    B, H, S, C = ql.shape
    Rr = qr.shape[3]
    V = wuv.shape[2]
    tq = _pick_tile(S, 128, SUBLANES)
    kb = _pick_tile(S, 512, LANES)
    return pl.pallas_call(
        functools.partial(_mla_prompt_kernel, tq=tq, kb=kb),
        grid=(B, S // tq),
        in_specs=[pl.BlockSpec((None, H, tq, C), lambda b, i: (b, 0, i, 0)),
                  pl.BlockSpec((None, H, tq, Rr), lambda b, i: (b, 0, i, 0)),
                  pl.BlockSpec((None, C, S), lambda b, i: (b, 0, 0)),
                  pl.BlockSpec((None, Rr, S), lambda b, i: (b, 0, 0)),
                  pl.BlockSpec((None, S, C), lambda b, i: (b, 0, 0)),
                  pl.BlockSpec((H, C, V), lambda b, i: (0, 0, 0))],
        out_specs=pl.BlockSpec((None, H, tq, V), lambda b, i: (b, 0, i, 0)),
        out_shape=jax.ShapeDtypeStruct((B, H, S, V), F32),
        scratch_shapes=[pltpu.VMEM((H, tq, 1), F32), pltpu.VMEM((H, tq, 1), F32), pltpu.VMEM((H * tq, C), F32)],
        compiler_params=_cparams(("parallel", "arbitrary"), 56 << 20),
        name="mla_prompt",
    )(ql, qr, ckvT, krT, ckv, wuv)


def _sb_logs(z):
    t = jnp.log1p(jnp.exp(-jnp.abs(z)))
    return -(jnp.maximum(z, 0.0) + t), jnp.minimum(z, 0.0) - t


def _suffix_excl(x, tri):
    hi = x.astype(BF16)
    lo = (x - hi.astype(F32)).astype(BF16)
    return jnp.dot(hi, tri, preferred_element_type=F32) + jnp.dot(lo, tri, preferred_element_type=F32)


def _tri(n):
    j = np.arange(n)
    return jnp.asarray((j[:, None] > j[None, :]).astype(np.float32), dtype=BF16)


def _sb_prompt_kernel(q_ref, kT_ref, v_ref, tri_ref, o_ref, c_scr, acc_scr, *, tq):
    R = SB_REP
    qi = pl.program_id(2)
    t0 = qi * tq
    q = q_ref[...].reshape(R * tq, SB_HD)
    ti = lax.broadcasted_iota(jnp.int32, (R * tq, 1), 0) % tq + t0
    c_scr[...] = jnp.zeros(c_scr.shape, F32)
    acc_scr[...] = jnp.zeros(acc_scr.shape, F32)
    tri = tri_ref[...]

    def body(step, carry):
        k0 = pl.multiple_of((qi - step) * tq, tq)
        z = jnp.dot(q, kT_ref[:, pl.ds(k0, tq)], preferred_element_type=F32)
        kpos = k0 + lax.broadcasted_iota(jnp.int32, (1, tq), 1)
        valid = kpos < ti
        log_rem, log_sig = _sb_logs(z)
        log_rem = jnp.where(valid, log_rem, 0.0)
        between = _suffix_excl(log_rem, tri) + c_scr[...]
        a = jnp.where(valid, jnp.exp(log_sig + between), 0.0)
        acc_scr[...] += jnp.dot(a.astype(BF16), v_ref[pl.ds(k0, tq), :], preferred_element_type=F32)
        c_scr[...] += log_rem.sum(-1, keepdims=True)
        return carry

    lax.fori_loop(0, qi + 1, body, 0)
    o_ref[...] = acc_scr[...].reshape(R, tq, SB_HD)


def _sb_prompt(q, kT, v):
    B, G, R, S, hd = q.shape
    tq = _pick_tile(S, 256, LANES)
    return pl.pallas_call(
        functools.partial(_sb_prompt_kernel, tq=tq),
        grid=(B, G, S // tq),
        in_specs=[pl.BlockSpec((None, None, R, tq, hd), lambda b, g, i: (b, g, 0, i, 0)),
                  pl.BlockSpec((None, None, hd, S), lambda b, g, i: (b, g, 0, 0)),
                  pl.BlockSpec((None, None, S, hd), lambda b, g, i: (b, g, 0, 0)),
                  pl.BlockSpec((tq, tq), lambda b, g, i: (0, 0))],
        out_specs=pl.BlockSpec((None, None, R, tq, hd), lambda b, g, i: (b, g, 0, i, 0)),
        out_shape=jax.ShapeDtypeStruct((B, G, R, S, hd), F32),
        scratch_shapes=[pltpu.VMEM((R * tq, 1), F32), pltpu.VMEM((R * tq, hd), F32)],
        compiler_params=_cparams(("parallel", "parallel", "arbitrary"), 40 << 20),
        name="sb_prompt",
    )(q, kT, v, _tri(tq))


def _page_copies(pt_ref, b, first_page, n_pages, src_of_page, dst_of_slot, sem, *, for_wait):
    copies = []
    for i in range(n_pages):
        page = 0 if for_wait else pt_ref[b, first_page + i]
        copies.append(pltpu.make_async_copy(src_of_page(page), dst_of_slot(i), sem))
    return copies


def _nsa_dec_cmp_kernel(pt_ref, cache_ref, q_ref, w_ref, e_ref, oc_ref, imp_ref, buf, hi_scr, lo_scr, sem, *, l, n_pages):
    b = pl.program_id(0)
    nb = pl.num_programs(0)
    slot = b % 2
    P = buf.shape[2] // n_pages

    def copies(bb, sl, for_wait):
        return _page_copies(pt_ref, bb, 0, n_pages, lambda pg: cache_ref.at[l, pg],
                            lambda i: buf.at[sl, :, pl.ds(i * P, P)], sem.at[sl], for_wait=for_wait)

    @pl.when(b == 0)
    def _():
        for c in copies(0, 0, False):
            c.start()

    @pl.when(b + 1 < nb)
    def _():
        for c in copies(b + 1, 1 - slot, False):
            c.start()

    for c in copies(b, slot, True):
        c.wait()

    w = w_ref[...]

    def split(i, carry):
        o = pl.multiple_of(i * P, P)
        prod = buf[slot, :, pl.ds(o, P)] * w
        hi = prod.astype(BF16)
        hi_scr[:, pl.ds(o, P)] = hi
        lo_scr[:, pl.ds(o, P)] = (prod - hi.astype(F32)).astype(BF16)
        return carry

    lax.fori_loop(0, n_pages, split, 0)
    e = e_ref[...]
    kvT = jnp.dot(hi_scr[...], e, preferred_element_type=F32) + jnp.dot(lo_scr[...], e, preferred_element_type=F32)
    half = kvT.shape[0] // 2
    kcT = kvT[:half].astype(BF16)
    vcT = kvT[half:].astype(BF16)
    q = q_ref[...].astype(BF16)
    sc = jnp.dot(q, kcT, preferred_element_type=F32)
    m = sc.max(-1, keepdims=True)
    ex = jnp.exp(sc - m)
    pc = ex / ex.sum(-1, keepdims=True)
    o = lax.dot_general(pc.astype(BF16), vcT, (((1,), (1,)), ((), ())), preferred_element_type=F32)
    head_g = lax.broadcasted_iota(jnp.int32, o.shape, 0) // NSA_REP
    lane_g = lax.broadcasted_iota(jnp.int32, o.shape, 1) // NSA_HD
    o = jnp.where(head_g == lane_g, o, 0.0)
    oc = o[:, 0:NSA_HD]
    for g in range(1, NSA_KV):
        oc = oc + o[:, g * NSA_HD:(g + 1) * NSA_HD]
    oc_ref[...] = oc
    imp_ref[...] = jnp.concatenate([pc[g * NSA_REP:(g + 1) * NSA_REP].sum(0, keepdims=True) for g in range(NSA_KV)], axis=0)


def _nsa_dec_cmp(page_table, cacheT, l, qbd, wT, e):
    NB, n_pages = page_table.shape
    C, P = cacheT.shape[2], cacheT.shape[3]
    past = n_pages * P
    NBK = e.shape[1]
    H = qbd.shape[1]
    return pl.pallas_call(
        functools.partial(_nsa_dec_cmp_kernel, l=l, n_pages=n_pages),
        grid_spec=pltpu.PrefetchScalarGridSpec(
            num_scalar_prefetch=1, grid=(NB,),
            in_specs=[pl.BlockSpec(memory_space=pl.ANY),
                      pl.BlockSpec((None, H, C // 2), lambda b, pt: (b, 0, 0)),
                      pl.BlockSpec((C, P), lambda b, pt: (0, 0)),
                      pl.BlockSpec((past, NBK), lambda b, pt: (0, 0))],
            out_specs=[pl.BlockSpec((None, H, NSA_HD), lambda b, pt: (b, 0, 0)),
                       pl.BlockSpec((None, NSA_KV, NBK), lambda b, pt: (b, 0, 0))],
            scratch_shapes=[pltpu.VMEM((2, C, past), F32), pltpu.VMEM((C, past), BF16), pltpu.VMEM((C, past), BF16),
                            pltpu.SemaphoreType.DMA((2,))]),
        out_shape=[jax.ShapeDtypeStruct((NB, H, NSA_HD), F32), jax.ShapeDtypeStruct((NB, NSA_KV, NBK), F32)],
        compiler_params=_cparams(("arbitrary",), 2 * C * past * 4 + 2 * C * past * 2 + 4 * past * NBK * 2 + (12 << 20)),
        name="nsa_dec_cmp",
    )(page_table, cacheT, qbd, wT, e)


def _topk_kernel(imp_ref, o_ref, *, n_pick):
    imp = imp_ref[...]
    n = imp.shape[1]
    j = lax.broadcasted_iota(jnp.int32, imp.shape, 1)
    jf = j.astype(F32)
    score = jnp.where((j == 0) | (j == n - 1), -jnp.inf, imp)
    out = jnp.zeros(imp.shape, F32)
    for r in range(n_pick):
        mx = score.max(-1, keepdims=True)
        jmin = jnp.where(score == mx, jf, float(n)).min(-1, keepdims=True)
        out = jnp.where(j == r, jmin, out)
        score = jnp.where(jf == jmin, -jnp.inf, score)
    o_ref[...] = out.astype(jnp.int32)


def _topk(imp, n_pick):
    return pl.pallas_call(
        functools.partial(_topk_kernel, n_pick=n_pick),
        out_shape=jax.ShapeDtypeStruct(imp.shape, jnp.int32),
        name="nsa_topk",
    )(imp)


def _softmax_with_new(s, mask, s_new):
    if mask is not None:
        s = s + (1.0 - mask) * NEG
    m = jnp.maximum(s.max(-1, keepdims=True), s_new)
    p = jnp.exp(s - m)
    if mask is not None:
        p = p * mask
    p_new = jnp.exp(s_new - m)
    return p, p_new, p.sum(-1, keepdims=True) + p_new


def _nsa_dec_selwin_kernel(pt_ref, idx_ref, cache_ref, win_ref, q_ref, nsel_ref, nwin_ref, g_ref, oc_ref, o_ref,
                           kbuf, vbuf, sem, *, l, n_blk):
    b = pl.program_id(0)
    nb = pl.num_programs(0)
    slot = b % 2
    P = cache_ref.shape[3]
    G, R, hd = NSA_KV, NSA_REP, NSA_HD
    bpp = P // NSA_BLOCK

    def copies(bb, sl, for_wait):
        out = []
        for g in range(G):
            for n in range(n_blk):
                page = 0 if for_wait else pt_ref[bb, idx_ref[bb, g * n_blk + n] // bpp]
                out.append(pltpu.make_async_copy(cache_ref.at[l, page, pl.ds(g * hd, hd), :],
                                                 kbuf.at[sl, g, :, pl.ds(n * P, P)], sem.at[sl]))
                out.append(pltpu.make_async_copy(cache_ref.at[l, page, pl.ds(G * hd + g * hd, hd), :],
                                                 vbuf.at[sl, g, :, pl.ds(n * P, P)], sem.at[sl]))
        return out

    @pl.when(b == 0)
    def _():
        for c in copies(0, 0, False):
            c.start()

    @pl.when(b + 1 < nb)
    def _():
        for c in copies(b + 1, 1 - slot, False):
            c.start()

    for c in copies(b, slot, True):
        c.wait()

    q = q_ref[...]
    qb = q.astype(BF16)
    H = q.shape[0]
    nsel = nsel_ref[...]
    nwin = nwin_ref[...]
    gates = g_ref[...]
    lane = lax.broadcasted_iota(jnp.int32, (1, P), 1)
    head_g = lax.broadcasted_iota(jnp.int32, (H, hd), 0) // R
    contract_last = (((1,), (1,)), ((), ()))
    o_s = jnp.zeros((H, hd), F32)
    o_w = jnp.zeros((H, hd), F32)
    for g in range(G):
        pieces = []
        for n in range(n_blk):
            which = idx_ref[b, g * n_blk + n] % bpp
            pieces.append(jnp.where((lane // NSA_BLOCK) == which, 1.0, 0.0))
        mask = jnp.concatenate(pieces, axis=1)
        s = jnp.dot(qb, kbuf[slot, g].astype(BF16), preferred_element_type=F32)
        k_new = nsel[:, g * hd:(g + 1) * hd]
        v_new = nsel[:, G * hd + g * hd:G * hd + (g + 1) * hd]
        s_new = (q * k_new).sum(-1, keepdims=True)
        p, p_new, den = _softmax_with_new(s, mask, s_new)
        o_sg = (lax.dot_general(p.astype(BF16), vbuf[slot, g].astype(BF16), contract_last, preferred_element_type=F32)
                + p_new * v_new) / den
        o_s = jnp.where(head_g == g, o_sg, o_s)
        kw = win_ref[g * hd:(g + 1) * hd, :].astype(BF16)
        vw = win_ref[G * hd + g * hd:G * hd + (g + 1) * hd, :].astype(BF16)
        sw = jnp.dot(qb, kw, preferred_element_type=F32)
        kw_new = nwin[:, g * hd:(g + 1) * hd]
        vw_new = nwin[:, G * hd + g * hd:G * hd + (g + 1) * hd]
        sw_new = (q * kw_new).sum(-1, keepdims=True)
        pw, pw_new, denw = _softmax_with_new(sw, None, sw_new)
        o_wg = (lax.dot_general(pw.astype(BF16), vw, contract_last, preferred_element_type=F32) + pw_new * vw_new) / denw
        o_w = jnp.where(head_g == g, o_wg, o_w)
    gl = lax.broadcasted_iota(jnp.int32, (H, gates.shape[1]), 1)
    hrow = lax.broadcasted_iota(jnp.int32, (H, gates.shape[1]), 0)
    gk = [jnp.where(gl == hrow * 3 + k, gates, 0.0).sum(-1, keepdims=True) for k in range(3)]
    o_ref[...] = oc_ref[...] * gk[0] + o_s * gk[1] + o_w * gk[2]


def _nsa_dec_selwin(page_table, idx, cacheT, winT, l, q, new_sel, new_win, gates, oc):
    NB = page_table.shape[0]
    C, P = cacheT.shape[2], cacheT.shape[3]
    W = winT.shape[3]
    H, hd = q.shape[1], q.shape[2]
    n_blk = idx.shape[1] // NSA_KV
    row3 = lambda n: pl.BlockSpec((None, 1, n), lambda b, pt, ix: (b, 0, 0))
    per_head = pl.BlockSpec((None, H, hd), lambda b, pt, ix: (b, 0, 0))
    return pl.pallas_call(
        functools.partial(_nsa_dec_selwin_kernel, l=l, n_blk=n_blk),
        grid_spec=pltpu.PrefetchScalarGridSpec(
            num_scalar_prefetch=2, grid=(NB,),
            in_specs=[pl.BlockSpec(memory_space=pl.ANY),
                      pl.BlockSpec((None, None, C, W), lambda b, pt, ix: (l, b, 0, 0)),
                      per_head, row3(C), row3(C), row3(3 * H), per_head],
            out_specs=per_head,
            scratch_shapes=[pltpu.VMEM((2, NSA_KV, hd, n_blk * P), F32), pltpu.VMEM((2, NSA_KV, hd, n_blk * P), F32),
                            pltpu.SemaphoreType.DMA((2,))]),
        out_shape=jax.ShapeDtypeStruct((NB, H, hd), F32),
        compiler_params=_cparams(("arbitrary",), 32 << 20),
        name="nsa_dec_selwin",
    )(page_table, idx, cacheT, winT, q, new_sel, new_win, gates, oc)


def _mla_dec_kernel(pt_ref, ckv_ref, krT_ref, ql_ref, qr_ref, cnew_ref, rnew_ref, o_ref, cbuf, rbuf, sem, *, l, n_pages):
    b = pl.program_id(0)
    nb = pl.num_programs(0)
    slot = b % 2
    P = ckv_ref.shape[2]

    def copies(bb, sl, for_wait):
        return (_page_copies(pt_ref, bb, 0, n_pages, lambda pg: ckv_ref.at[l, pg],
                             lambda i: cbuf.at[sl, pl.ds(i * P, P), :], sem.at[0, sl], for_wait=for_wait)
                + _page_copies(pt_ref, bb, 0, n_pages, lambda pg: krT_ref.at[l, pg],
                               lambda i: rbuf.at[sl, :, pl.ds(i * P, P)], sem.at[1, sl], for_wait=for_wait))

    @pl.when(b == 0)
    def _():
        for c in copies(0, 0, False):
            c.start()

    @pl.when(b + 1 < nb)
    def _():
        for c in copies(b + 1, 1 - slot, False):
            c.start()

    for c in copies(b, slot, True):
        c.wait()

    scale = (MLA_NOPE + MLA_ROPE) ** -0.5
    ql = ql_ref[...]
    qr = qr_ref[...]
    ckv = cbuf[slot].astype(BF16)
    s = (lax.dot_general(ql.astype(BF16), ckv, (((1,), (1,)), ((), ())), preferred_element_type=F32)
         + jnp.dot(qr.astype(BF16), rbuf[slot].astype(BF16), preferred_element_type=F32)) * scale
    c_new = cnew_ref[...]
    s_new = ((ql * c_new).sum(-1, keepdims=True) + (qr * rnew_ref[...]).sum(-1, keepdims=True)) * scale
    p, p_new, den = _softmax_with_new(s, None, s_new)
    o_ref[...] = (jnp.dot(p.astype(BF16), ckv, preferred_element_type=F32) + p_new * c_new) / den


def _mla_dec(page_table, ckv_cache, krT_cache, l, ql, qr, c_new, r_new):
    NB, n_pages = page_table.shape
    P, C = ckv_cache.shape[2], ckv_cache.shape[3]
    Rr = krT_cache.shape[2]
    H = ql.shape[1]
    past = n_pages * P
    blk = lambda a, b_: pl.BlockSpec((None, a, b_), lambda b, pt: (b, 0, 0))
    return pl.pallas_call(
        functools.partial(_mla_dec_kernel, l=l, n_pages=n_pages),
        grid_spec=pltpu.PrefetchScalarGridSpec(
            num_scalar_prefetch=1, grid=(NB,),
            in_specs=[pl.BlockSpec(memory_space=pl.ANY), pl.BlockSpec(memory_space=pl.ANY),
                      blk(H, C), blk(H, Rr), blk(1, C), blk(1, Rr)],
            out_specs=blk(H, C),
            scratch_shapes=[pltpu.VMEM((2, past, C), F32), pltpu.VMEM((2, Rr, past), F32), pltpu.SemaphoreType.DMA((2, 2))]),
        out_shape=jax.ShapeDtypeStruct((NB, H, C), F32),
        compiler_params=_cparams(("arbitrary",), 2 * past * C * 4 + 2 * Rr * past * 4 + past * C * 2 + (12 << 20)),
        name="mla_dec",
    )(page_table, ckv_cache, krT_cache, ql, qr, c_new, r_new)


def _sb_dec_kernel(pt_ref, cache_ref, q_ref, tri_ref, o_ref, kbuf, vbuf, c_scr, acc_scr, sem, *, l, ppc, n_chunks):
    b = pl.program_id(0)
    c = pl.program_id(1)
    step = b * n_chunks + c
    total = pl.num_programs(0) * n_chunks
    slot = step % 2
    P = cache_ref.shape[3]
    half = cache_ref.shape[2] // 2
    H = q_ref.shape[0]

    def copies(st, sl, for_wait):
        bb = st // n_chunks
        first_page = (n_chunks - 1 - st % n_chunks) * ppc
        return (_page_copies(pt_ref, bb, first_page, ppc, lambda pg: cache_ref.at[l, pg, pl.ds(0, half), :],
                             lambda i: kbuf.at[sl, :, pl.ds(i * P, P)], sem.at[0, sl], for_wait=for_wait)
                + _page_copies(pt_ref, bb, first_page, ppc, lambda pg: cache_ref.at[l, pg, pl.ds(half, half), :],
                               lambda i: vbuf.at[sl, :, pl.ds(i * P, P)], sem.at[1, sl], for_wait=for_wait))

    @pl.when(step == 0)
    def _():
        for cp in copies(0, 0, False):
            cp.start()

    @pl.when(step + 1 < total)
    def _():
        for cp in copies(step + 1, 1 - slot, False):
            cp.start()

    for cp in copies(step, slot, True):
        cp.wait()

    @pl.when(c == 0)
    def _():
        c_scr[...] = jnp.zeros(c_scr.shape, F32)
        acc_scr[...] = jnp.zeros(acc_scr.shape, F32)

    z = jnp.dot(q_ref[...].astype(BF16), kbuf[slot].astype(BF16), preferred_element_type=F32)
    log_rem, log_sig = _sb_logs(z)
    nsub = z.shape[1] // P
    stacked = jnp.concatenate([log_rem[:, k * P:(k + 1) * P] for k in range(nsub)], axis=0)
    within = _suffix_excl(stacked, tri_ref[...])
    totals = stacked.sum(-1, keepdims=True)
    carry = c_scr[...]
    pieces = [None] * nsub
    for k in range(nsub - 1, -1, -1):
        pieces[k] = within[k * H:(k + 1) * H] + carry
        carry = carry + totals[k * H:(k + 1) * H]
    c_scr[...] = carry
    a = jnp.exp(log_sig + jnp.concatenate(pieces, axis=1))
    acc_scr[...] += lax.dot_general(a.astype(BF16), vbuf[slot].astype(BF16), (((1,), (1,)), ((), ())),
                                    preferred_element_type=F32)

    @pl.when(c == n_chunks - 1)
    def _():
        o = acc_scr[...]
        head_g = lax.broadcasted_iota(jnp.int32, o.shape, 0) // SB_REP
        lane_g = lax.broadcasted_iota(jnp.int32, o.shape, 1) // SB_HD
        o = jnp.where(head_g == lane_g, o, 0.0)
        out = o[:, 0:SB_HD]
        for g in range(1, SB_KV):
            out = out + o[:, g * SB_HD:(g + 1) * SB_HD]
        o_ref[...] = out


def _sb_dec(page_table, cacheT, l, qbd):
    NB, n_pages = page_table.shape
    C, P = cacheT.shape[2], cacheT.shape[3]
    H = qbd.shape[1]
    ppc = _pick_tile(n_pages, 32, 1)
    n_chunks = n_pages // ppc
    half = C // 2
    return pl.pallas_call(
        functools.partial(_sb_dec_kernel, l=l, ppc=ppc, n_chunks=n_chunks),
        grid_spec=pltpu.PrefetchScalarGridSpec(
            num_scalar_prefetch=1, grid=(NB, n_chunks),
            in_specs=[pl.BlockSpec(memory_space=pl.ANY),
                      pl.BlockSpec((None, H, half), lambda b, c, pt: (b, 0, 0)),
                      pl.BlockSpec((P, P), lambda b, c, pt: (0, 0))],
            out_specs=pl.BlockSpec((None, H, SB_HD), lambda b, c, pt: (b, 0, 0)),
            scratch_shapes=[pltpu.VMEM((2, half, ppc * P), F32), pltpu.VMEM((2, half, ppc * P), F32),
                            pltpu.VMEM((H, 1), F32), pltpu.VMEM((H, half), F32), pltpu.SemaphoreType.DMA((2, 2))]),
        out_shape=jax.ShapeDtypeStruct((NB, H, SB_HD), F32),
        compiler_params=_cparams(("arbitrary", "arbitrary"), 4 * half * ppc * P * 4 + 2 * half * ppc * P * 2 + (12 << 20)),
        name="sb_dec",
    )(page_table, cacheT, qbd, _tri(P))


def _rope_tables(pos0, n, d):
    half = d // 2
    inv = np.exp(np.arange(half, dtype=np.float64) * (-2.0 * math.log(ROPE_THETA) / d))
    ang = (pos0 + np.arange(n, dtype=np.float64))[:, None] * inv[None, :]
    return jnp.asarray(np.cos(ang), F32), jnp.asarray(np.sin(ang), F32)


def _rope(x, cos, sin):
    half = x.shape[-1] // 2
    shape = (x.shape[0],) + (1,) * (x.ndim - 2) + (half,)
    c, s = cos.reshape(shape), sin.reshape(shape)
    x1, x2 = x[..., :half], x[..., half:]
    return jnp.concatenate([x1 * c - x2 * s, x1 * s + x2 * c], axis=-1)


def _rope_kv(kv, cos, sin):
    T, C = kv.shape
    k = _rope(kv[:, :C // 2].reshape(T, -1, NSA_HD), cos, sin).reshape(T, C // 2)
    return jnp.concatenate([k, kv[:, C // 2:]], axis=-1)


def _block_diag_q(q, groups):
    N, H, hd = q.shape
    rep = H // groups
    onehot = (jnp.arange(H)[:, None] // rep == jnp.arange(groups)[None, :]).astype(q.dtype)
    return (q[:, :, None, :] * onehot[None, :, :, None]).reshape(N, H, groups * hd)


def _layer_weights(l, d_model, w_in, w_uq, w_uk, w_uv, w_cmp, w_branch, w_out, w_xq, w_xkv, w_xo, w_up, w_down):
    sizes, src, dst, n_pad = _in_layout(d_model)
    cols = []
    for sz, s in zip(sizes, src):
        piece = w_in[l][:, s:s + sz]
        pad = _round_up(sz, LANES) - sz
        cols.append(piece if pad == 0 else jnp.pad(piece, ((0, 0), (0, pad))))
    w_in_p = jnp.concatenate(cols, axis=1).astype(BF16)
    wfull = jnp.broadcast_to(w_cmp[l][:, :, None, :], (2, NSA_BLOCK, NSA_KV, NSA_HD))
    wfull = jnp.transpose(wfull, (1, 0, 2, 3)).reshape(NSA_BLOCK, 2 * NSA_KV * NSA_HD)
    return dict(
        w_in=w_in_p,
        w_uq=w_uq[l].astype(BF16),
        w_uk=jnp.transpose(w_uk[l], (1, 2, 0)).astype(BF16),
        w_uv=jnp.transpose(w_uv[l], (1, 0, 2)).astype(BF16),
        wfull=wfull,
        w_branch=w_branch[l].astype(BF16),
        w_out=w_out[l].astype(BF16),
        w_xq=w_xq[l].astype(BF16),
        w_xkv=w_xkv[l].astype(BF16),
        w_xo=w_xo[l].astype(BF16),
        w_up=w_up[l].astype(BF16),
        w_down=w_down[l].astype(BF16),
    )


def _split_in(h, d_model):
    sizes, _, dst, _ = _in_layout(d_model)
    return [h[:, d:d + sz] for sz, d in zip(sizes, dst)]


def _mla_queries(b_cq, norm_cq_l, lw, cos_r, sin_r):
    M = b_cq.shape[0]
    qm = _linear(b_cq, lw['w_uq'], gain=norm_cq_l, name="mla_uq").reshape(M, MLA_HEADS, MLA_NOPE + MLA_ROPE)
    q_nope = jnp.transpose(qm[..., :MLA_NOPE], (1, 0, 2))
    q_rope = _rope(qm[..., MLA_NOPE:], cos_r, sin_r)
    q_lat = _bmm(q_nope, lw['w_uk'], name="mla_absorb_k")
    return q_lat, jnp.transpose(q_rope, (1, 0, 2))


def _ffn_tail(x, u, gated, lw):
    return _linear(gated, lw['w_down'], residual=x, name="ffn_down")


def _prompt_layer(x, lw, lp, mem_prompt, tabs):
    B, S, D = x.shape
    M = B * S
    x2 = x.reshape(M, D)
    cos_a, sin_a, cos_r, sin_r = tabs
    tile_b = lambda t: jnp.tile(t, (B, 1))
    cos_a, sin_a, cos_r, sin_r = tile_b(cos_a), tile_b(sin_a), tile_b(cos_r), tile_b(sin_r)

    h = _linear(x2, lw['w_in'], gain=lp['norm_mix'], name="in_proj")
    a_q, a_cmp, a_sel, a_win, a_gate, b_cq, b_ckv, b_kr, c_q, c_kv, gate = _split_in(h, D)

    G, R, hd = NSA_KV, NSA_REP, NSA_HD
    qa = _rope(a_q.reshape(M, NSA_HEADS, hd), cos_a, sin_a) * (hd ** -0.5)
    qa = jnp.transpose(qa.reshape(B, S, G, R, hd), (0, 2, 3, 1, 4)).astype(BF16)
    kv_c = _rope_kv(a_cmp, cos_a, sin_a)
    kv_s = _rope_kv(a_sel, cos_a, sin_a)
    kv_w = _rope_kv(a_win, cos_a, sin_a)
    comp = _nsa_compress(kv_c.reshape(B, S, 2 * G * hd), lw['wfull']).reshape(B, S // NSA_BLOCK, 2, G, hd)
    kcT = jnp.transpose(comp[:, :, 0], (0, 2, 3, 1)).astype(BF16)
    vc = jnp.transpose(comp[:, :, 1], (0, 2, 1, 3)).astype(BF16)

    def kT_v(kv):
        kv5 = kv.reshape(B, S, 2, G, hd)
        return (jnp.transpose(kv5[:, :, 0], (0, 2, 3, 1)).astype(BF16), jnp.transpose(kv5[:, :, 1], (0, 2, 1, 3)).astype(BF16))

    ksT, vs = kT_v(kv_s)
    kwT, vw = kT_v(kv_w)
    ga = jax.nn.sigmoid(a_gate).reshape(B, S, G, R * 3)
    ga = jnp.transpose(ga, (0, 2, 1, 3))
    o_a = _nsa_prompt(qa, kcT, vc, ksT, vs, kwT, vw, ga)
    o_a = jnp.transpose(o_a, (0, 3, 1, 2, 4)).reshape(M, BRANCH_W)

    q_lat, q_rope = _mla_queries(b_cq, lp['norm_cq'], lw, cos_r, sin_r)
    ckv = _rmsnorm(b_ckv, lp['norm_ckv'], name="norm_ckv")
    kr = _rope(b_kr, cos_r, sin_r)
    H = MLA_HEADS
    ql = jnp.transpose(q_lat.reshape(H, B, S, MLA_KV_RANK), (1, 0, 2, 3)).astype(BF16)
    qr = jnp.transpose(q_rope.reshape(H, B, S, MLA_ROPE), (1, 0, 2, 3)).astype(BF16)
    ckv3 = ckv.reshape(B, S, MLA_KV_RANK)
    o_b = _mla_prompt(ql, qr, jnp.transpose(ckv3, (0, 2, 1)).astype(BF16),
                      jnp.transpose(kr.reshape(B, S, MLA_ROPE), (0, 2, 1)).astype(BF16),
                      ckv3.astype(BF16), lw['w_uv'])
    o_b = jnp.transpose(o_b, (0, 2, 1, 3)).reshape(M, BRANCH_W)

    Gs, Rs = SB_KV, SB_REP
    qc = (c_q * (SB_HD ** -0.5)).reshape(B, S, Gs, Rs, SB_HD)
    qc = jnp.transpose(qc, (0, 2, 3, 1, 4)).astype(BF16)
    kv_sb = c_kv.reshape(B, S, 2, Gs, SB_HD)
    o_c = _sb_prompt(qc, jnp.transpose(kv_sb[:, :, 0], (0, 2, 3, 1)).astype(BF16),
                     jnp.transpose(kv_sb[:, :, 1], (0, 2, 1, 3)).astype(BF16))
    o_c = jnp.transpose(o_c, (0, 3, 1, 2, 4)).reshape(M, BRANCH_W)

    x2 = _merge(o_a, o_b, o_c, gate, x2, lw['w_branch'], lw['w_out'])

    Nm = mem_prompt.shape[1]
    mem_kv = _linear(mem_prompt.reshape(B * Nm, D), lw['w_xkv'], gain=lp['norm_mem'], name="mem_kv")
    mem_kv = mem_kv.reshape(B, Nm, 2, X_HEADS, X_HD)
    x3 = _cross_prompt(x2.reshape(B, S, D), lp['norm_cross'], lw['w_xq'],
                       jnp.transpose(mem_kv[:, :, 0], (0, 2, 3, 1)).astype(BF16),
                       jnp.transpose(mem_kv[:, :, 1], (0, 2, 1, 3)).astype(BF16), lw['w_xo']).reshape(M, D)

    u = _linear(x3, lw['w_up'], gain=lp['norm_ffn'], name="ffn_up")
    gated = _convgate_prompt(u, lp['conv_w'], lp['conv_b'], S)
    x4 = _linear(gated, lw['w_down'], residual=x3, name="ffn_down")

    wlen = min(WINDOW, S)
    state = dict(
        cmp=kv_c.reshape(B, S, 2, G, hd), sel=kv_s.reshape(B, S, 2, G, hd),
        win=kv_w.reshape(B, S, 2, G, hd)[:, S - wlen:],
        ckv=ckv3, kr=kr.reshape(B, S, MLA_ROPE), sb=kv_sb, mem=mem_kv,
        conv=u.reshape(B, S, -1)[:, S - (CONV_W - 1):])
    return x4.reshape(B, S, D), state


def _decode_layer(x, l, lw, lp, caches, page_table, tabs):
    NB, D = x.shape
    n_pages = page_table.shape[1]
    cmpT, selT, winT, ckv_cache, krT_cache, sbT, mem, state_win_l, state_conv_l = caches
    P = cmpT.shape[3]
    past = n_pages * P
    cos_a, sin_a, cos_r, sin_r = (jnp.broadcast_to(t, (NB, t.shape[1])) for t in tabs)

    h = _linear(x, lw['w_in'], gain=lp['norm_mix'], name="in_proj_dec")
    a_q, a_cmp, a_sel, a_win, a_gate, b_cq, b_ckv, b_kr, c_q, c_kv, gate = _split_in(h, D)

    G, R, hd, H = NSA_KV, NSA_REP, NSA_HD, NSA_HEADS
    qa = _rope(a_q.reshape(NB, H, hd), cos_a, sin_a) * (hd ** -0.5)
    kv_c = _rope_kv(a_cmp, cos_a, sin_a)
    kv_s = _rope_kv(a_sel, cos_a, sin_a)
    kv_w = _rope_kv(a_win, cos_a, sin_a)
    NBK = past // NSA_BLOCK
    wT = jnp.tile(jnp.transpose(lw['wfull']), (1, P // NSA_BLOCK))
    e = (jnp.arange(past)[:, None] // NSA_BLOCK == jnp.arange(NBK)[None, :]).astype(BF16)
    oc, imp = _nsa_dec_cmp(page_table, cmpT, l, _block_diag_q(qa, G), wT, e)
    n_sel = min(NSA_TOPN, NBK + 1)
    n_pick = n_sel - 3
    picks = _topk(imp.reshape(NB * G, NBK), n_pick)[:, :n_pick].reshape(NB, G, n_pick)
    forced = jnp.broadcast_to(jnp.asarray([0, NBK - 1], jnp.int32), (NB, G, 2))
    idx = jnp.concatenate([forced, picks], axis=2).reshape(NB, G * (n_pick + 2))
    ga = jax.nn.sigmoid(a_gate)
    o_a = _nsa_dec_selwin(page_table, idx, selT, winT, l, qa, kv_s[:, None, :], kv_w[:, None, :], ga[:, None, :], oc)
    o_a = o_a.reshape(NB, BRANCH_W)

    q_lat, q_rope = _mla_queries(b_cq, lp['norm_cq'], lw, cos_r, sin_r)
    ckv = _rmsnorm(b_ckv, lp['norm_ckv'], name="norm_ckv_dec")
    kr = _rope(b_kr, cos_r, sin_r)
    o_lat = _mla_dec(page_table, ckv_cache, krT_cache, l, jnp.transpose(q_lat, (1, 0, 2)), jnp.transpose(q_rope, (1, 0, 2)),
                     ckv[:, None, :], kr[:, None, :])
    o_b = _bmm(jnp.transpose(o_lat, (1, 0, 2)), lw['w_uv'], name="mla_up_v")
    o_b = jnp.transpose(o_b, (1, 0, 2)).reshape(NB, BRANCH_W)

    qc = (c_q * (SB_HD ** -0.5)).reshape(NB, SB_HEADS, SB_HD)
    o_c = _sb_dec(page_table, sbT, l, _block_diag_q(qc, SB_KV)).reshape(NB, BRANCH_W)

    x2 = _merge(o_a, o_b, o_c, gate, x, lw['w_branch'], lw['w_out'])

    q_x = _linear(x2, lw['w_xq'], gain=lp['norm_cross'], name="cross_q_dec").reshape(NB, X_HEADS, X_HD)
    o_x = _cross_decode(q_x, mem, l).reshape(NB, X_HEADS * X_HD)
    x3 = _linear(o_x, lw['w_xo'], residual=x2, name="cross_o_dec")

    u = _linear(x3, lw['w_up'], gain=lp['norm_ffn'], name="ffn_up_dec")
    gated = _convgate_decode(u, state_conv_l.reshape(NB, -1), lp['conv_w'], lp['conv_b'])
    x4 = _linear(gated, lw['w_down'], residual=x3, name="ffn_down_dec")

    full_w = jnp.concatenate([state_win_l, kv_w.reshape(NB, 1, 2, G, hd)], axis=1)
    wlen = min(WINDOW, full_w.shape[1])
    state = dict(
        cmp=kv_c.reshape(NB, 1, 2, G, hd), sel=kv_s.reshape(NB, 1, 2, G, hd), win=full_w[:, full_w.shape[1] - wlen:],
        ckv=ckv[:, None, :], kr=kr[:, None, :], sb=c_kv.reshape(NB, 1, 2, SB_KV, SB_HD),
        conv=jnp.concatenate([state_conv_l[:, 1:], u[:, None, :]], axis=1))
    return x4, state


def kernel(x_prompt, x_sample, cache_nsa_cmp, cache_nsa_sel, state_nsa_win, cache_mla_ckv, cache_mla_krope, cache_sb_kv, cache_mem_kv, state_conv, page_table, mem_prompt, norm_mix, w_in, norm_cq, w_uq, norm_ckv, w_uk, w_uv, w_cmp, w_branch, w_out, norm_cross, norm_mem, w_xq, w_xkv, w_xo, norm_ffn, w_up, conv_w, conv_b, w_down, norm_final):
    B, S, D = x_prompt.shape
    NB = x_sample.shape[0]
    assert x_sample.shape[1] == 1
    depth = w_in.shape[0]
    L, NP, P = cache_nsa_cmp.shape[:3]
    n_pages = page_table.shape[1]
    past = n_pages * P

    cmpT = jnp.transpose(cache_nsa_cmp, (0, 1, 3, 4, 5, 2)).reshape(L, NP, -1, P)
    selT = jnp.transpose(cache_nsa_sel, (0, 1, 3, 4, 5, 2)).reshape(L, NP, -1, P)
    sbT = jnp.transpose(cache_sb_kv, (0, 1, 3, 4, 5, 2)).reshape(L, NP, -1, P)
    krT = jnp.transpose(cache_mla_krope, (0, 1, 3, 2))
    winT = jnp.transpose(state_nsa_win, (0, 1, 3, 4, 5, 2)).reshape(L, NB, 2 * NSA_KV * NSA_HD, -1)
    mem = cache_mem_kv.reshape(L, NB, -1, X_HD)

    tabs_p = _rope_tables(0, S, NSA_HD) + _rope_tables(0, S, MLA_ROPE)
    tabs_s = _rope_tables(past, 1, NSA_HD) + _rope_tables(past, 1, MLA_ROPE)

    xp, xs = x_prompt, x_sample.reshape(NB, D)
    st_p, st_s = [], []
    for l in range(depth):
        lw = _layer_weights(l, D, w_in, w_uq, w_uk, w_uv, w_cmp, w_branch, w_out, w_xq, w_xkv, w_xo, w_up, w_down)
        lp = dict(norm_mix=norm_mix[l], norm_cq=norm_cq[l], norm_ckv=norm_ckv[l], norm_cross=norm_cross[l],
                  norm_mem=norm_mem[l], norm_ffn=norm_ffn[l], conv_w=conv_w[l], conv_b=conv_b[l])
        xp, sp = _prompt_layer(xp, lw, lp, mem_prompt, tabs_p)
        caches = (cmpT, selT, winT, cache_mla_ckv, krT, sbT, mem, state_nsa_win[l], state_conv[l])
        xs, ss = _decode_layer(xs, l, lw, lp, caches, page_table, tabs_s)
        st_p.append(sp)
        st_s.append(ss)

    y_prompt = _rmsnorm(xp.reshape(B * S, D), norm_final, name="final_norm_p").reshape(B, S, D)
    y_sample = _rmsnorm(xs, norm_final, name="final_norm_s").reshape(NB, 1, D)
    stack = lambda sts, name: jnp.stack([s[name] for s in sts])
    return (y_prompt, y_sample,
            stack(st_p, 'cmp'), stack(st_s, 'cmp'), stack(st_p, 'sel'), stack(st_s, 'sel'),
            stack(st_p, 'win'), stack(st_s, 'win'), stack(st_p, 'ckv'), stack(st_s, 'ckv'),
            stack(st_p, 'kr'), stack(st_s, 'kr'), stack(st_p, 'sb'), stack(st_s, 'sb'),
            stack(st_p, 'mem'), stack(st_p, 'conv'), stack(st_s, 'conv'))
```
